```python
import jax
import jax.numpy as jnp
from jax import lax
import numpy as np

D_MODEL = 1024
BATCH = 16
SEQ = 4096
DEPTH = 1
DEC_BATCH = 128
DEC_SEQ = 1
PAST_LEN = 8192
PAGE_SIZE = 128

H_A = 4
DK_A = 128
DV_A = 128
CONV_W = 4
GDN_CHUNK = 64
CONV_DIM = H_A * (2 * DK_A + DV_A)
H_B = 8
HD_B = 64
MOBA_BLOCK = 256
MOBA_TOPK = 3
Q_BLOCK = 128
PROJ_SIZES = (H_A * DK_A, H_A * DK_A, H_A * DV_A, H_A * DV_A, H_A, H_A, H_B * HD_B, H_B * HD_B, H_B * HD_B)
D_IN = sum(PROJ_SIZES)
N_GROUPS = 4
E_PER_GROUP = 8
N_EXPERTS = N_GROUPS * E_PER_GROUP
TOPK_E = 2
D_EXPERT = 512
MOE_BLOCK = 64
DN_ALPHA = (2 * DEPTH) ** 0.25
DN_BETA = (8 * DEPTH) ** -0.25
LN_EPS = 1e-5
NORM_EPS = 1e-6

kernel_name = 'hybrid_gdn_moba_hmoe_step'


def layer_norm(x, g, b):
    xf = x.astype(jnp.float32)
    mu = xf.mean(-1, keepdims=True)
    var = jnp.square(xf - mu).mean(-1, keepdims=True)
    return ((xf - mu) * lax.rsqrt(var + LN_EPS) * g + b).astype(x.dtype)


def l2_normalize(x):
    xf = x.astype(jnp.float32)
    return (xf * lax.rsqrt(jnp.sum(xf * xf, -1, keepdims=True) + NORM_EPS)).astype(x.dtype)


def in_projection(x, w_in):
    h = jnp.einsum('btd,de->bte', x, w_in)
    return jnp.split(h, [int(s) for s in np.cumsum(PROJ_SIZES)[:-1]], axis=-1)


def short_conv(u, conv_buf, conv_w):
    T = u.shape[1]
    up = jnp.concatenate([conv_buf, u], axis=1)
    y = sum(conv_w[j] * up[:, j:j + T] for j in range(CONV_W))
    return jax.nn.silu(y), up[:, T:]


def gated_delta_rule(q, k, v, g, beta, S0):
    f32 = jnp.float32
    B, T, H, DK = q.shape
    DV = v.shape[-1]
    C = min(GDN_CHUNK, T)
    Tp = -(-T // C) * C
    n = Tp // C

    def prep(a):
        a = jnp.pad(a, [(0, 0), (0, Tp - T)] + [(0, 0)] * (a.ndim - 2))
        a = a.reshape((B, n, C) + a.shape[2:])
        return jnp.moveaxis(a, 3, 1).astype(f32)

    qc = prep(q) * DK ** -0.5
    kc = prep(k)
    vc = prep(v)
    bc = prep(beta)
    gc = jnp.cumsum(prep(g), axis=-1)
    idx = jnp.arange(C)
    lower_incl = idx[:, None] >= idx[None, :]
    strict = idx[:, None] > idx[None, :]
    decay_mat = jnp.exp(jnp.where(lower_incl, gc[..., :, None] - gc[..., None, :], -jnp.inf))
    kbeta = kc * bc[..., None]
    a_mat = jnp.where(strict, jnp.einsum('bhnid,bhnjd->bhnij', kbeta, kc) * decay_mat, 0.0)
    t_mat = a_mat + jnp.eye(C, dtype=f32)
    w_val = lax.linalg.triangular_solve(t_mat, vc * bc[..., None], left_side=True, lower=True, unit_diagonal=True)
    w_key = lax.linalg.triangular_solve(t_mat, kbeta * jnp.exp(gc)[..., None], left_side=True, lower=True, unit_diagonal=True)
    attn = jnp.einsum('bhnid,bhnjd->bhnij', qc, kc) * decay_mat
    q_dec = qc * jnp.exp(gc)[..., None]
    k_dec = kc * jnp.exp(gc[..., -1:] - gc)[..., None]
    g_last = jnp.exp(gc[..., -1])

    def step(S, xs):
        wv, wk, at, qd, kd, gl = xs
        v_new = wv - jnp.einsum('bhcd,bhde->bhce', wk, S)
        o = jnp.einsum('bhcd,bhde->bhce', qd, S) + jnp.einsum('bhij,bhje->bhie', at, v_new)
        S = S * gl[..., None, None] + jnp.einsum('bhcd,bhce->bhde', kd, v_new)
        return S, o

    xs = tuple(jnp.moveaxis(a, 2, 0) for a in (w_val, w_key, attn, q_dec, k_dec, g_last))
    S_final, o = lax.scan(step, S0.astype(f32), xs)
    o = jnp.moveaxis(o, 0, 2).reshape(B, H, Tp, DV)[:, :, :T]
    return jnp.moveaxis(o, 1, 2).astype(v.dtype), S_final.astype(S0.dtype)


def gdn_branch(q_raw, k_raw, v_raw, z, a_raw, b_raw, conv_buf, S0, conv_w, a_log, dt_bias, gdn_norm):
    f32 = jnp.float32
    B, T, _ = q_raw.shape
    qkv, new_buf = short_conv(jnp.concatenate([q_raw, k_raw, v_raw], -1), conv_buf, conv_w)
    q, k, v = jnp.split(qkv, [H_A * DK_A, 2 * H_A * DK_A], axis=-1)
    q = l2_normalize(q.reshape(B, T, H_A, DK_A))
    k = l2_normalize(k.reshape(B, T, H_A, DK_A))
    v = v.reshape(B, T, H_A, DV_A)
    g = -jnp.exp(a_log.astype(f32)) * jax.nn.softplus(a_raw.astype(f32) + dt_bias.astype(f32))
    beta = jax.nn.sigmoid(b_raw.astype(f32))
    o, S_new = gated_delta_rule(q, k, v, g, beta, S0)
    of = o.astype(f32)
    of = of * lax.rsqrt(jnp.mean(of * of, -1, keepdims=True) + NORM_EPS) * gdn_norm
    y = of * jax.nn.silu(z.reshape(B, T, H_A, DV_A).astype(f32))
    return y.astype(z.dtype).reshape(B, T, H_A * DV_A), new_buf, S_new


def moba_attention(q, k_full, v_full, offset, slopes):
    f32 = jnp.float32
    B, Tq, H, hd = q.shape
    NB = k_full.shape[1] // MOBA_BLOCK
    ksel = min(MOBA_TOPK, NB)
    Qc = min(Q_BLOCK, Tq)
    nC = Tq // Qc
    kb = k_full.reshape(B, NB, MOBA_BLOCK, H, hd)
    vb = v_full.reshape(B, NB, MOBA_BLOCK, H, hd)
    kmean = kb.astype(f32).mean(axis=2)
    qs = q.reshape(B, nC, Qc, H, hd)
    scale = hd ** -0.5
    h_idx = jnp.arange(H)[:, None, None, None]
    r_idx = jnp.arange(MOBA_BLOCK)

    def one(args):
        b, c = args
        qc = qs[b, c].astype(f32)
        start = offset + c * Qc
        pos = start + jnp.arange(Qc)
        own = start // MOBA_BLOCK
        gate = jnp.einsum('qhd,nhd->hqn', qc, kmean[b])
        gate = jnp.where(jnp.arange(NB) < own, gate, -jnp.inf)
        _, sel = lax.top_k(gate, ksel)
        valid = sel < own
        kbb = kb[b]
        vbb = vb[b]
        ks = kbb[sel[..., None], r_idx, h_idx].astype(f32)
        vs = vbb[sel[..., None], r_idx, h_idx].astype(f32)
        kpos_sel = sel[..., None] * MOBA_BLOCK + r_idx
        s_sel = jnp.einsum('qhd,hqjkd->hqjk', qc, ks) * scale
        s_sel = s_sel - slopes[:, None, None, None] * (pos[None, :, None, None] - kpos_sel).astype(f32)
        s_sel = jnp.where(valid[..., None], s_sel, -jnp.inf)
        k_own = lax.dynamic_index_in_dim(kbb, own, axis=0, keepdims=False).astype(f32)
        v_own = lax.dynamic_index_in_dim(vbb, own, axis=0, keepdims=False).astype(f32)
        kpos_own = own * MOBA_BLOCK + r_idx
        dist = (pos[:, None] - kpos_own[None, :]).astype(f32)
        s_own = jnp.einsum('qhd,khd->hqk', qc, k_own) * scale - slopes[:, None, None] * dist[None]
        s_own = jnp.where((dist >= 0)[None], s_own, -jnp.inf)
        s = jnp.concatenate([s_sel.reshape(H, Qc, ksel * MOBA_BLOCK), s_own], axis=-1)
        p = jax.nn.softmax(s, axis=-1)
        p_sel = p[..., :ksel * MOBA_BLOCK].reshape(H, Qc, ksel, MOBA_BLOCK)
        p_own = p[..., ksel * MOBA_BLOCK:]
        o = jnp.einsum('hqjk,hqjkd->qhd', p_sel, vs) + jnp.einsum('hqk,khd->qhd', p_own, v_own)
        return o.astype(q.dtype)

    bi = jnp.repeat(jnp.arange(B), nC)
    ci = jnp.tile(jnp.arange(nC), B)
    out = lax.map(one, (bi, ci))
    return out.reshape(B, Tq, H, hd)


def mixer_output(x, y_a, y_b, w_br_a, w_br_b, w_merge, b_merge, w_out, ln_g, ln_b):
    gates = jax.nn.sigmoid(jnp.einsum('btd,de->bte', x, w_merge) + b_merge)
    g_a, g_b = jnp.split(gates, 2, axis=-1)
    m = g_a * jnp.einsum('bte,ed->btd', y_a, w_br_a) + g_b * jnp.einsum('bte,ed->btd', y_b, w_br_b)
    return layer_norm(DN_ALPHA * x + jnp.einsum('btd,de->bte', m, w_out), ln_g, ln_b)


def expert_mlp(xb, e, w_exp_gate, w_exp_up, w_exp_down):
    h = jax.nn.silu(xb @ w_exp_gate[e]) * (xb @ w_exp_up[e])
    return h @ w_exp_down[e]


def hierarchical_moe(x, w_rg, b_rg, w_re, b_re, w_exp_gate, w_exp_up, w_exp_down):
    f32 = jnp.float32
    N, D = x.shape
    lg = (x @ w_rg + b_rg).astype(f32)
    p_group = jax.nn.softmax(lg, axis=-1)
    grp = jnp.argmax(lg, axis=-1)
    le_all = jnp.einsum('nd,gde->nge', x, w_re) + b_re
    le = jnp.take_along_axis(le_all, grp[:, None, None], axis=1)[:, 0].astype(f32)
    top_val, top_idx = lax.top_k(le, TOPK_E)
    wts = jax.nn.softmax(top_val, axis=-1) * jnp.take_along_axis(p_group, grp[:, None], axis=1)
    eid = (grp[:, None] * E_PER_GROUP + top_idx).reshape(-1)
    tok = jnp.repeat(jnp.arange(N), TOPK_E)
    wa = wts.reshape(-1)
    A = N * TOPK_E
    order = jnp.argsort(eid)
    e_s = eid[order]
    tok_s = tok[order]
    w_s = wa[order]
    counts = jnp.bincount(eid, length=N_EXPERTS)
    start = jnp.cumsum(counts) - counts
    pcounts = (counts + MOE_BLOCK - 1) // MOE_BLOCK * MOE_BLOCK
    pend = jnp.cumsum(pcounts)
    pstart = pend - pcounts
    dest = pstart[e_s] + jnp.arange(A) - start[e_s]
    n_blocks = A // MOE_BLOCK + N_EXPERTS + 1
    P = n_blocks * MOE_BLOCK
    buf_tok = jnp.zeros((P,), jnp.int32).at[dest].set(tok_s)
    buf_w = jnp.zeros((P,), f32).at[dest].set(w_s)
    blk_e = jnp.minimum(jnp.searchsorted(pend, jnp.arange(n_blocks) * MOE_BLOCK, side='right'), N_EXPERTS - 1)
    xb = x[buf_tok].reshape(n_blocks, MOE_BLOCK, D)
    yb = lax.map(lambda a: expert_mlp(a[0], a[1], w_exp_gate, w_exp_up, w_exp_down), (xb, blk_e))
    y = jnp.zeros((N, D), f32).at[buf_tok].add(yb.reshape(P, D).astype(f32) * buf_w[:, None])
    return y.astype(x.dtype)


def setup_inputs(seed: int = 0) -> dict:
    key = jax.random.key(seed)
    ks = jax.random.split(key, 32)
    f32 = jnp.float32
    n_pages = PAST_LEN // PAGE_SIZE
    n_pool = (DEC_BATCH * n_pages * 5) // 4

    def nrm(k, shape, s):
        return jax.random.normal(k, shape, f32) * s

    col_scale = jnp.asarray(np.concatenate([np.full(s, DN_BETA if i in (2, 8) else 1.0, np.float32) for i, s in enumerate(PROJ_SIZES)]))
    page_table = jax.random.permutation(ks[3], n_pool)[:DEC_BATCH * n_pages].reshape(DEC_BATCH, n_pages).astype(jnp.int32)
    dt = jnp.exp(jax.random.uniform(ks[9], (DEPTH, H_A), f32, np.log(1e-3), np.log(1e-1)))
    return {
        'x_prompt': nrm(ks[0], (BATCH, SEQ, D_MODEL), 1.0),
        'x_sample': nrm(ks[1], (DEC_BATCH, DEC_SEQ, D_MODEL), 1.0),
        'cache_k': nrm(ks[2], (DEPTH, n_pool, PAGE_SIZE, H_B, HD_B), 1.0),
        'cache_v': nrm(ks[4], (DEPTH, n_pool, PAGE_SIZE, H_B, HD_B), DN_BETA),
        'page_table': page_table,
        'state_conv': nrm(ks[5], (DEPTH, DEC_BATCH, CONV_W - 1, CONV_DIM), 1.0),
        'state_gdn': nrm(ks[6], (DEPTH, DEC_BATCH, H_A, DK_A, DV_A), 0.1),
        'w_in': nrm(ks[7], (DEPTH, D_MODEL, D_IN), D_MODEL ** -0.5) * col_scale,
        'conv_w': nrm(ks[8], (DEPTH, CONV_W, CONV_DIM), CONV_W ** -0.5),
        'a_log': jnp.log(jax.random.uniform(ks[10], (DEPTH, H_A), f32, 1.0, 16.0)),
        'dt_bias': jnp.log(jnp.expm1(dt)),
        'gdn_norm': 1.0 + nrm(ks[11], (DEPTH, DV_A), 0.02),
        'w_br_a': nrm(ks[12], (DEPTH, H_A * DV_A, D_MODEL), (H_A * DV_A) ** -0.5),
        'w_br_b': nrm(ks[13], (DEPTH, H_B * HD_B, D_MODEL), (H_B * HD_B) ** -0.5),
        'w_merge': nrm(ks[14], (DEPTH, D_MODEL, 2 * D_MODEL), D_MODEL ** -0.5),
        'b_merge': nrm(ks[15], (DEPTH, 2 * D_MODEL), 0.1),
        'w_out': nrm(ks[16], (DEPTH, D_MODEL, D_MODEL), DN_BETA * D_MODEL ** -0.5),
        'ln1_g': 1.0 + nrm(ks[17], (DEPTH, D_MODEL), 0.02),
        'ln1_b': nrm(ks[18], (DEPTH, D_MODEL), 0.02),
        'w_router_group': nrm(ks[19], (DEPTH, D_MODEL, N_GROUPS), D_MODEL ** -0.5),
        'b_router_group': nrm(ks[20], (DEPTH, N_GROUPS), 0.01),
        'w_router_expert': nrm(ks[21], (DEPTH, N_GROUPS, D_MODEL, E_PER_GROUP), D_MODEL ** -0.5),
        'b_router_expert': nrm(ks[22], (DEPTH, N_GROUPS, E_PER_GROUP), 0.01),
        'w_exp_gate': nrm(ks[23], (DEPTH, N_EXPERTS, D_MODEL, D_EXPERT), D_MODEL ** -0.5),
        'w_exp_up': nrm(ks[24], (DEPTH, N_EXPERTS, D_MODEL, D_EXPERT), D_MODEL ** -0.5),
        'w_exp_down': nrm(ks[25], (DEPTH, N_EXPERTS, D_EXPERT, D_MODEL), DN_BETA * D_EXPERT ** -0.5),
        'ln2_g': 1.0 + nrm(ks[26], (DEPTH, D_MODEL), 0.02),
        'ln2_b': nrm(ks[27], (DEPTH, D_MODEL), 0.02),
    }


def reference(x_prompt, x_sample, cache_k, cache_v, page_table, state_conv, state_gdn,
              w_in, conv_w, a_log, dt_bias, gdn_norm, w_br_a, w_br_b, w_merge, b_merge, w_out,
              ln1_g, ln1_b, w_router_group, b_router_group, w_router_expert, b_router_expert,
              w_exp_gate, w_exp_up, w_exp_down, ln2_g, ln2_b):
    B, T, D = x_prompt.shape
    Bs, Ts, _ = x_sample.shape
    past = page_table.shape[1] * PAGE_SIZE
    slopes = jnp.exp2(-8.0 * jnp.arange(1, H_B + 1, dtype=jnp.float32) / H_B)
    xp, xs = x_prompt, x_sample
    k_prompt_rows, v_prompt_rows, conv_prompt, gdn_prompt = [], [], [], []
    k_sample_rows, v_sample_rows, conv_sample, gdn_sample = [], [], [], []
    for l in range(DEPTH):
        qa, ka, va, za, aa, ba, qb, kbp, vbp = in_projection(xp, w_in[l])
        conv0 = jnp.zeros((B, CONV_W - 1, CONV_DIM), xp.dtype)
        S0 = jnp.zeros((B, H_A, DK_A, DV_A), xp.dtype)
        ya_p, conv_p, S_p = gdn_branch(qa, ka, va, za, aa, ba, conv0, S0, conv_w[l], a_log[l], dt_bias[l], gdn_norm[l])
        k_p = kbp.reshape(B, T, H_B, HD_B)
        v_p = vbp.reshape(B, T, H_B, HD_B)
        pad_p = -(-T // MOBA_BLOCK) * MOBA_BLOCK - T
        yb_p = moba_attention(qb.reshape(B, T, H_B, HD_B),
                              jnp.pad(k_p, [(0, 0), (0, pad_p), (0, 0), (0, 0)]),
                              jnp.pad(v_p, [(0, 0), (0, pad_p), (0, 0), (0, 0)]),
                              0, slopes).reshape(B, T, H_B * HD_B)
        xp1 = mixer_output(xp, ya_p, yb_p, w_br_a[l], w_br_b[l], w_merge[l], b_merge[l], w_out[l], ln1_g[l], ln1_b[l])
        qa, ka, va, za, aa, ba, qb, kbs, vbs = in_projection(xs, w_in[l])
        ya_s, conv_s, S_s = gdn_branch(qa, ka, va, za, aa, ba, state_conv[l], state_gdn[l], conv_w[l], a_log[l], dt_bias[l], gdn_norm[l])
        k_s = kbs.reshape(Bs, Ts, H_B, HD_B)
        v_s = vbs.reshape(Bs, Ts, H_B, HD_B)
        t_all = past + Ts
        tail = jnp.zeros((Bs, -(-t_all // MOBA_BLOCK) * MOBA_BLOCK - t_all, H_B, HD_B), k_s.dtype)
        k_full = jnp.concatenate([cache_k[l, page_table].reshape(Bs, past, H_B, HD_B), k_s, tail], axis=1)
        v_full = jnp.concatenate([cache_v[l, page_table].reshape(Bs, past, H_B, HD_B), v_s, tail], axis=1)
        yb_s = moba_attention(qb.reshape(Bs, Ts, H_B, HD_B), k_full, v_full, past, slopes).reshape(Bs, Ts, H_B * HD_B)
        xs1 = mixer_output(xs, ya_s, yb_s, w_br_a[l], w_br_b[l], w_merge[l], b_merge[l], w_out[l], ln1_g[l], ln1_b[l])
        h = jnp.concatenate([xp1.reshape(B * T, D), xs1.reshape(Bs * Ts, D)], axis=0)
        moe = hierarchical_moe(h, w_router_group[l], b_router_group[l], w_router_expert[l], b_router_expert[l],
                               w_exp_gate[l], w_exp_up[l], w_exp_down[l])
        h2 = layer_norm(DN_ALPHA * h + moe, ln2_g[l], ln2_b[l])
        xp = h2[:B * T].reshape(B, T, D)
        xs = h2[B * T:].reshape(Bs, Ts, D)
        k_prompt_rows.append(k_p)
        v_prompt_rows.append(v_p)
        conv_prompt.append(conv_p)
        gdn_prompt.append(S_p)
        k_sample_rows.append(k_s)
        v_sample_rows.append(v_s)
        conv_sample.append(conv_s)
        gdn_sample.append(S_s)
    return (xp, xs, jnp.stack(k_prompt_rows), jnp.stack(v_prompt_rows), jnp.stack(conv_prompt), jnp.stack(gdn_prompt),
            jnp.stack(k_sample_rows), jnp.stack(v_sample_rows), jnp.stack(conv_sample), jnp.stack(gdn_sample))
```

```python
import functools

import numpy as np
import jax
import jax.numpy as jnp
from jax import lax
from jax.experimental import pallas as pl
from jax.experimental.pallas import tpu as pltpu

F32 = jnp.float32
BF16 = jnp.bfloat16
I32 = jnp.int32
HI = lax.Precision.HIGHEST
NEG_INF = float("-inf")

LANES = 128
SUBLANES = 8
VMEM_LIMIT = 56 * 1024 * 1024

H_A, DK_A, DV_A, CONV_W = 4, 128, 128, 4
H_B, HD_B, MOBA_BLOCK, MOBA_TOPK = 8, 64, 256, 3
PAGE_SIZE = 128
N_GROUPS, E_PER_GROUP, TOPK_E = 4, 8, 2
N_EXPERTS = N_GROUPS * E_PER_GROUP
LN_EPS = 1e-5
NORM_EPS = 1e-6
GDN_CHUNK = 64
MOE_ROWS = 256


def _cparams(sem):
    return pltpu.CompilerParams(dimension_semantics=sem, vmem_limit_bytes=VMEM_LIMIT)


def _dot(a, b, precision=None):
    return jnp.dot(a, b, preferred_element_type=F32, precision=precision)


def _dot_nt(a, b, precision=None):
    return lax.dot_general(a, b, (((1,), (1,)), ((), ())), preferred_element_type=F32, precision=precision)


def _dot_tn(a, b, precision=None):
    return lax.dot_general(a, b, (((0,), (0,)), ((), ())), preferred_element_type=F32, precision=precision)


def _sigmoid(x):
    return 1.0 / (1.0 + jnp.exp(-x))


def _silu(x):
    return x * _sigmoid(x)


def _softplus(x):
    return jnp.maximum(x, 0.0) + jnp.log(1.0 + jnp.exp(-jnp.abs(x)))


def _in_proj_kernel(x_ref, wm_ref, wab_ref, qkv_ref, z_ref, ab_ref, qb_ref, kb_ref, vb_ref):
    x = x_ref[...]
    xb = x.astype(BF16)
    qkv_ref[...] = _dot(xb, wm_ref[:, 0:1536])
    z_ref[...] = _dot(xb, wm_ref[:, 1536:2048])
    qb_ref[...] = _dot(xb, wm_ref[:, 2048:2560])
    kb_ref[...] = _dot(xb, wm_ref[:, 2560:3072])
    vb_ref[...] = _dot(xb, wm_ref[:, 3072:3584])
    ab_ref[...] = _dot(x, wab_ref[...], precision=HI)


def _in_proj(x2d, w_main, w_ab, tm):
    m, d = x2d.shape
    assert m % tm == 0
    row = lambda n: pl.BlockSpec((tm, n), lambda i: (i, 0))
    full = lambda a: pl.BlockSpec(a.shape, lambda i: (0, 0))
    widths = (1536, 512, LANES, 512, 512, 512)
    return pl.pallas_call(
        _in_proj_kernel,
        grid=(m // tm,),
        in_specs=[row(d), full(w_main), full(w_ab)],
        out_specs=[row(n) for n in widths],
        out_shape=[jax.ShapeDtypeStruct((m, n), F32) for n in widths],
        compiler_params=_cparams(("parallel",)),
        name="in_proj",
    )(x2d, w_main, w_ab)


def _gdn_prompt_kernel(q_ref, k_ref, v_ref, z_ref, ab_ref, cwq_ref, cwk_ref, cwv_ref, alog_ref, dtb_ref,
                       gnorm_ref, y_ref, s_out_ref, ext_ref, s_ref, *, chunk):
    h = pl.program_id(1)
    t = pl.program_id(2)
    tc = q_ref.shape[1]
    c = chunk

    @pl.when(t == 0)
    def _():
        ext_ref[:, 0:SUBLANES, :] = jnp.zeros((3, SUBLANES, LANES), F32)
        s_ref[...] = jnp.zeros_like(s_ref)

    def conv(idx, u_ref, cw_ref):
        u = u_ref[0]
        ext_ref[idx, SUBLANES:SUBLANES + tc, :] = u
        w = cw_ref[...]
        y = w[3:4] * u
        for j in range(CONV_W - 1):
            off = SUBLANES - (CONV_W - 1) + j
            y = y + w[j:j + 1] * ext_ref[idx, off:off + tc, :]
        ext_ref[idx, 0:SUBLANES, :] = u[tc - SUBLANES:tc]
        return _silu(y)

    def l2n(a):
        return a * lax.rsqrt(jnp.sum(a * a, -1, keepdims=True) + NORM_EPS)

    q = l2n(conv(0, q_ref, cwq_ref)) * (DK_A ** -0.5)
    k = l2n(conv(1, k_ref, cwk_ref))
    v = conv(2, v_ref, cwv_ref)

    ab = ab_ref[0]
    gfull = -jnp.exp(alog_ref[...]) * _softplus(ab + dtb_ref[...])
    bfull = _sigmoid(ab)
    lane = lax.broadcasted_iota(I32, (tc, LANES), 1)
    beta = jnp.sum(jnp.where(lane == h + H_A, bfull, 0.0), -1, keepdims=True)
    hsel_c = lax.broadcasted_iota(I32, (c, LANES), 1) == h

    ri = lax.broadcasted_iota(I32, (c, c), 0)
    ci = lax.broadcasted_iota(I32, (c, c), 1)
    lower_incl = ri >= ci
    strict = ri > ci
    eye = ri == ci
    l_incl = jnp.where(lower_incl, 1.0, 0.0).astype(F32)
    eye_f = jnp.where(eye, 1.0, 0.0).astype(F32)

    s = s_ref[...]
    outs = []
    for n in range(tc // c):
        sl = slice(n * c, (n + 1) * c)
        qc, kc, vc, bc = q[sl], k[sl], v[sl], beta[sl]
        gcum = _dot(l_incl, gfull[sl], precision=HI)
        gc_col = jnp.sum(jnp.where(hsel_c, gcum, 0.0), -1, keepdims=True)
        gc_row = jnp.sum(jnp.where(eye, gc_col, 0.0), 0, keepdims=True)
        decay = jnp.exp(jnp.where(lower_incl, gc_col - gc_row, NEG_INF))
        kbeta = kc * bc
        a_mat = jnp.where(strict, _dot_nt(kbeta, kc, precision=HI) * decay, 0.0)
        t_inv = eye_f - a_mat
        p = a_mat
        for _ in range(int(np.log2(c)) - 1):
            p = _dot(p, p, precision=HI)
            t_inv = t_inv + _dot(t_inv, p, precision=HI)
        egc = jnp.exp(gc_col)
        w_val = _dot(t_inv, vc * bc, precision=HI)
        w_key = _dot(t_inv, kbeta * egc, precision=HI)
        attn = _dot_nt(qc, kc, precision=HI) * decay
        gl = gc_col[c - 1:c]
        q_dec = qc * egc
        k_dec = kc * jnp.exp(gl - gc_col)
        v_new = w_val - _dot(w_key, s, precision=HI)
        outs.append(_dot(q_dec, s, precision=HI) + _dot(attn, v_new, precision=HI))
        s = s * jnp.exp(gl) + _dot_tn(k_dec, v_new, precision=HI)
    s_ref[...] = s
    o = jnp.concatenate(outs, axis=0) if len(outs) > 1 else outs[0]
    o = o * lax.rsqrt(jnp.mean(o * o, -1, keepdims=True) + NORM_EPS) * gnorm_ref[...]
    y_ref[0] = o * _silu(z_ref[0])

    @pl.when(t == pl.num_programs(2) - 1)
    def _():
        s_out_ref[0, 0] = s


def _gdn_prompt(qkv, z, ab, conv_w, alog_row, dtb_row, gnorm_row, tc):
    b, t, _ = qkv.shape
    assert t % tc == 0 and tc % GDN_CHUNK == 0
    col = lambda off: pl.BlockSpec((1, tc, LANES), lambda bi, hi, ti: (bi, ti, hi + off))
    cw = lambda off: pl.BlockSpec((CONV_W, LANES), lambda bi, hi, ti: (0, hi + off))
    vec = pl.BlockSpec((1, LANES), lambda bi, hi, ti: (0, 0))
    return pl.pallas_call(
        functools.partial(_gdn_prompt_kernel, chunk=GDN_CHUNK),
        grid=(b, H_A, t // tc),
        in_specs=[col(0), col(H_A), col(2 * H_A), col(0),
                  pl.BlockSpec((1, tc, LANES), lambda bi, hi, ti: (bi, ti, 0)),
                  cw(0), cw(H_A), cw(2 * H_A), vec, vec, vec],
        out_specs=[col(0), pl.BlockSpec((1, 1, DK_A, DV_A), lambda bi, hi, ti: (bi, hi, 0, 0))],
        out_shape=[jax.ShapeDtypeStruct((b, t, H_A * DV_A), F32),
                   jax.ShapeDtypeStruct((b, H_A, DK_A, DV_A), F32)],
        scratch_shapes=[pltpu.VMEM((3, tc + SUBLANES, LANES), F32), pltpu.VMEM((DK_A, DV_A), F32)],
        compiler_params=_cparams(("parallel", "parallel", "arbitrary")),
        name="gdn_prompt",
    )(qkv, qkv, qkv, z, ab, conv_w, conv_w, conv_w, alog_row, dtb_row, gnorm_row)


def _gdn_sample_kernel(u_ref, cbuf_ref, z_ref, ab_ref, s0_ref, cw_ref, alog_ref, dtb_ref, gnorm_ref,
                       y_ref, s_out_ref, cnew_ref):
    u = u_ref[0]
    cb = cbuf_ref[0]
    w = cw_ref[...]
    y = w[3:4] * u
    for j in range(CONV_W - 1):
        y = y + w[j:j + 1] * cb[j:j + 1]
    y = _silu(y)
    cnew_ref[0, 0:2, :] = cb[1:3]
    cnew_ref[0, 2:3, :] = u

    ab = ab_ref[0]
    gfull = -jnp.exp(alog_ref[...]) * _softplus(ab + dtb_ref[...])
    bfull = _sigmoid(ab)
    lane = lax.broadcasted_iota(I32, (1, LANES), 1)
    zrow = z_ref[0]
    rows8 = lax.broadcasted_iota(I32, (SUBLANES, LANES), 0) == 0
    outs = []
    for h in range(H_A):
        qh = y[:, h * DK_A:(h + 1) * DK_A]
        kh = y[:, (H_A + h) * DK_A:(H_A + h + 1) * DK_A]
        vh = y[:, (2 * H_A + h) * DK_A:(2 * H_A + h + 1) * DK_A]
        qh = qh * lax.rsqrt(jnp.sum(qh * qh, -1, keepdims=True) + NORM_EPS) * (DK_A ** -0.5)
        kh = kh * lax.rsqrt(jnp.sum(kh * kh, -1, keepdims=True) + NORM_EPS)
        g = jnp.sum(jnp.where(lane == h, gfull, 0.0), -1, keepdims=True)
        bt = jnp.sum(jnp.where(lane == h + H_A, bfull, 0.0), -1, keepdims=True)
        eg = jnp.exp(g)
        s0 = s0_ref[0, h]
        q8 =jnp.where(rows8, qh, 0.0)
        k8 = jnp.where(rows8, kh, 0.0)
        ks = _dot(k8, s0, precision=HI)[0:1]
        qs = _dot(q8, s0, precision=HI)[0:1]
        v_new = bt * (vh - eg * ks)
        o = eg * qs + jnp.sum(qh * kh, -1, keepdims=True) * v_new
        v8 = jnp.where(rows8, v_new, 0.0)
        s_out_ref[0, h] = s0 * eg + _dot_tn(k8, v8, precision=HI)
        o = o * lax.rsqrt(jnp.mean(o * o, -1, keepdims=True) + NORM_EPS) * gnorm_ref[...]
        outs.append(o * _silu(zrow[:, h * DV_A:(h + 1) * DV_A]))
    y_ref[0] = jnp.concatenate(outs, axis=-1)


def _gdn_sample(qkv, z, ab, state_conv, state_gdn, conv_w, alog_row, dtb_row, gnorm_row):
    bs = qkv.shape[0]
    cdim = qkv.shape[-1]
    r3 = lambda n, w: pl.BlockSpec((1, n, w), lambda i: (i, 0, 0))
    vec = pl.BlockSpec((1, LANES), lambda i: (0, 0))
    st = pl.BlockSpec((1, H_A, DK_A, DV_A), lambda i: (i, 0, 0, 0))
    return pl.pallas_call(
        _gdn_sample_kernel,
        grid=(bs,),
        in_specs=[r3(1, cdim), r3(CONV_W - 1, cdim), r3(1, H_A * DV_A), r3(1, LANES), st,
                  pl.BlockSpec((CONV_W, cdim), lambda i: (0, 0)), vec, vec, vec],
        out_specs=[r3(1, H_A * DV_A), st, r3(CONV_W - 1, cdim)],
        out_shape=[jax.ShapeDtypeStruct((bs, 1, H_A * DV_A), F32),
                   jax.ShapeDtypeStruct(state_gdn.shape, F32),
                   jax.ShapeDtypeStruct(state_conv.shape, F32)],
        compiler_params=_cparams(("parallel",)),
        name="gdn_sample",
    )(qkv.reshape(bs, 1, cdim), state_conv, z.reshape(bs, 1, -1), ab.reshape(bs, 1, LANES), state_gdn,
      conv_w, alog_row, dtb_row, gnorm_row)


def _top_mask(gate, lane, k):
    sel = jnp.zeros(gate.shape, F32)
    for _ in range(k):
        m = jnp.max(gate, -1, keepdims=True)
        idx = jnp.min(jnp.where(gate == m, lane, LANES), -1, keepdims=True)
        pick = lane == idx
        sel = jnp.where(pick, 1.0, sel)
        gate = jnp.where(pick, NEG_INF, gate)
    return sel


def _moba_prompt_kernel(slopes_ref, q_ref, k_ref, v_ref, o_ref, kmean_ref, *, nb):
    hp = pl.program_id(1)
    i = pl.program_id(2)
    blk = MOBA_BLOCK

    @pl.when(i == 0)
    def _():
        kmean_ref[...] = jnp.zeros_like(kmean_ref)
        for n in range(nb):
            kmean_ref[n:n + 1, :] = jnp.mean(k_ref[0, n * blk:(n + 1) * blk, :], axis=0, keepdims=True)

    q = q_ref[0]
    lane = lax.broadcasted_iota(I32, (blk, LANES), 1)
    rel = (lax.broadcasted_iota(I32, (blk, blk), 0) - lax.broadcasted_iota(I32, (blk, blk), 1)).astype(F32)
    row0 = pl.multiple_of(i * blk, blk)
    k_own = k_ref[0, pl.ds(row0, blk), :].astype(BF16)
    v_own = v_ref[0, pl.ds(row0, blk), :].astype(BF16)
    kmean = kmean_ref[...]
    out = jnp.zeros((blk, LANES), F32)
    for hh in range(2):
        slope = slopes_ref[hp * 2 + hh]
        head_lanes = (lane >= hh * HD_B) & (lane < (hh + 1) * HD_B)
        qh = jnp.where(head_lanes, q, 0.0)
        gate = _dot_nt(qh, kmean, precision=HI)
        gate = jnp.where(lane < i, gate, NEG_INF)
        sel = jnp.where(lane < i, _top_mask(gate, lane, MOBA_TOPK), 0.0)
        qs = (qh * (HD_B ** -0.5)).astype(BF16)

        s = _dot_nt(qs, k_own) - slope * rel
        s = jnp.where(rel >= 0.0, s, NEG_INF)
        m0 = jnp.max(s, -1, keepdims=True)
        p = jnp.exp(s - m0)
        l0 = jnp.sum(p, -1, keepdims=True)
        acc0 = _dot(p.astype(BF16), v_own)

        def body(j, carry):
            m, l, acc = carry
            selj = jnp.sum(jnp.where(lane == j, sel, 0.0), -1, keepdims=True) > 0.5
            rj = pl.multiple_of(j * blk, blk)
            kj = k_ref[0, pl.ds(rj, blk), :].astype(BF16)
            vj = v_ref[0, pl.ds(rj, blk), :].astype(BF16)
            sj = _dot_nt(qs, kj) - slope * (rel + ((i - j) * blk).astype(F32))
            sj = jnp.where(selj, sj, NEG_INF)
            m_new = jnp.maximum(m, jnp.max(sj, -1, keepdims=True))
            a = jnp.exp(m - m_new)
            pj = jnp.exp(sj - m_new)
            return m_new, a * l + jnp.sum(pj, -1, keepdims=True), a * acc + _dot(pj.astype(BF16), vj)

        m, l, acc = lax.fori_loop(0, i, body, (m0, l0, acc0))
        out = jnp.where(head_lanes, acc / l, out)
    o_ref[0] = out


def _moba_prompt(qb, kb, vb, slopes):
    b, t, w = qb.shape
    nb = t // MOBA_BLOCK
    assert t % MOBA_BLOCK == 0 and nb <= LANES
    qspec = pl.BlockSpec((1, MOBA_BLOCK, LANES), lambda bi, hi, ti, sl: (bi, ti, hi))
    kvspec = pl.BlockSpec((1, t, LANES), lambda bi, hi, ti, sl: (bi, 0, hi))
    return pl.pallas_call(
        functools.partial(_moba_prompt_kernel, nb=nb),
        grid_spec=pltpu.PrefetchScalarGridSpec(
            num_scalar_prefetch=1,
            grid=(b, w // LANES, nb),
            in_specs=[qspec, kvspec, kvspec],
            out_specs=qspec,
            scratch_shapes=[pltpu.VMEM((LANES, LANES), F32)]),
        out_shape=jax.ShapeDtypeStruct((b, t, w), F32),
        compiler_params=_cparams(("parallel", "parallel", "arbitrary")),
        name="moba_prompt",
    )(slopes, qb, kb, vb)


PAGES_PER_STEP = 8
PAGES_PER_BLOCK = MOBA_BLOCK // PAGE_SIZE


def _moba_sample_select_kernel(pt_ref, q_ref, *refs):
    pages = refs[:PAGES_PER_STEP]
    sel_ref = refs[PAGES_PER_STEP]
    ksum_ref = refs[PAGES_PER_STEP + 1]
    s = pl.program_id(1)
    nblk = PAGES_PER_STEP // PAGES_PER_BLOCK
    rows = []
    for n in range(nblk):
        acc = jnp.sum(pages[n * PAGES_PER_BLOCK][0], axis=0, keepdims=True)
        for p in range(1, PAGES_PER_BLOCK):
            acc = acc + jnp.sum(pages[n * PAGES_PER_BLOCK + p][0], axis=0, keepdims=True)
        rows.append(acc)
    width = rows[0].shape[-1]
    rows.append(jnp.zeros((SUBLANES - nblk, width), F32))
    ksum_ref[s] = jnp.concatenate(rows, axis=0)

    @pl.when(s == pl.num_programs(1) - 1)
    def _():
        nsteps = ksum_ref.shape[0]
        kmean = ksum_ref[...].reshape(nsteps * SUBLANES, width) * (1.0 / MOBA_BLOCK)
        prod = kmean * q_ref[0]
        ci = lax.broadcasted_iota(I32, (width, LANES), 0)
        hi = lax.broadcasted_iota(I32, (width, LANES), 1)
        head_of = jnp.where(ci // HD_B == hi, 1.0, 0.0).astype(F32)
        gate = _dot(prod, head_of, precision=HI)
        r = lax.broadcasted_iota(I32, gate.shape, 0)
        live = (r % SUBLANES) < nblk
        blk_id = (r // SUBLANES) * nblk + (r % SUBLANES)
        gate = jnp.where(live, gate, NEG_INF)
        big = nsteps * SUBLANES * 4
        picks = []
        for _ in range(MOBA_TOPK):
            m = jnp.max(gate, 0, keepdims=True)
            idx = jnp.min(jnp.where(gate == m, blk_id, big), 0, keepdims=True)
            picks.append(idx)
            gate = jnp.where(blk_id == idx, NEG_INF, gate)
        picks.append(jnp.zeros((SUBLANES - MOBA_TOPK, LANES), I32))
        sel_ref[0] = jnp.concatenate(picks, axis=0)


def _moba_sample_select(page_table, q3, cache_k2):
    bs, n_pages = page_table.shape
    width = cache_k2.shape[-1]
    nsteps = n_pages // PAGES_PER_STEP
    assert n_pages % PAGES_PER_STEP == 0

    def page_spec(p):
        return pl.BlockSpec((1, PAGE_SIZE, width), lambda bi, si, pt: (pt[bi, si * PAGES_PER_STEP + p], 0, 0))

    return pl.pallas_call(
        _moba_sample_select_kernel,
        grid_spec=pltpu.PrefetchScalarGridSpec(
            num_scalar_prefetch=1,
            grid=(bs, nsteps),
            in_specs=[pl.BlockSpec((1, 1, width), lambda bi, si, pt: (bi, 0, 0))]
                     + [page_spec(p) for p in range(PAGES_PER_STEP)],
            out_specs=pl.BlockSpec((1, SUBLANES, LANES), lambda bi, si, pt: (bi, 0, 0)),
            scratch_shapes=[pltpu.VMEM((nsteps, SUBLANES, width), F32)]),
        out_shape=jax.ShapeDtypeStruct((bs, SUBLANES, LANES), I32),
        compiler_params=_cparams(("parallel", "arbitrary")),
        name="moba_sample_select",
    )(page_table, q3, *([cache_k2] * PAGES_PER_STEP))


N_SEL_PAGES = MOBA_TOPK * PAGES_PER_BLOCK


def _moba_sample_attn_kernel(pg_ref, blk_ref, slopes_ref, q_ref, knew_ref, vnew_ref, *refs, past):
    kp = refs[:N_SEL_PAGES]
    vp = refs[N_SEL_PAGES:2 * N_SEL_PAGES]
    o_ref = refs[2 * N_SEL_PAGES]
    b = pl.program_id(0)
    h = pl.program_id(1)
    slope = slopes_ref[h]
    lane = lax.broadcasted_iota(I32, (SUBLANES, LANES), 1)
    half = h % 2
    head_lanes = (lane >= half * HD_B) & (lane < (half + 1) * HD_B)
    row0 = lax.broadcasted_iota(I32, (SUBLANES, LANES), 0) == 0
    live = head_lanes & row0
    q8 = jnp.where(live, q_ref[0], 0.0)
    qs = (q8 * (HD_B ** -0.5)).astype(BF16)
    kn = jnp.where(live, knew_ref[0], 0.0)
    vn = jnp.where(live, vnew_ref[0], 0.0)
    s_own = jnp.sum(_dot_nt(qs, kn.astype(BF16)), -1, keepdims=True)
    s_own = s_own[0:1]
    key_lane = lax.broadcasted_iota(I32, (1, PAGE_SIZE), 1)
    scores = []
    m = s_own
    for n in range(N_SEL_PAGES):
        blk = blk_ref[(b * H_B + h) * MOBA_TOPK + n // PAGES_PER_BLOCK]
        kpos = blk * MOBA_BLOCK + (n % PAGES_PER_BLOCK) * PAGE_SIZE + key_lane
        sn = _dot_nt(qs, kp[n][0].astype(BF16))[0:1] - slope * (past - kpos).astype(F32)
        scores.append(sn)
        m = jnp.maximum(m, jnp.max(sn, -1, keepdims=True))
    p_own = jnp.exp(s_own - m)
    l = p_own
    acc = p_own * vn[0:1]
    for n in range(N_SEL_PAGES):
        pn = jnp.exp(scores[n] - m)
        l = l + jnp.sum(pn, -1, keepdims=True)
        p8 = jnp.where(lax.broadcasted_iota(I32, (SUBLANES, PAGE_SIZE), 0) == 0, pn, 0.0)
        acc = acc + _dot(p8.astype(BF16), vp[n][0].astype(BF16))[0:1]
    o_ref[0, 0] = jnp.where(head_lanes[0:1], acc / l, 0.0)


def _moba_sample_attn(pages, blocks, slopes, q3, knew3, vnew3, cache_k2, cache_v2, past):
    bs = q3.shape[0]

    def page_spec(n):
        return pl.BlockSpec((1, PAGE_SIZE, LANES),
                            lambda bi, hi, pg, bl, sl: (pg[(bi * H_B + hi) * N_SEL_PAGES + n], 0, hi // 2))

    row = pl.BlockSpec((1, 1, LANES), lambda bi, hi, pg, bl, sl: (bi, 0, hi // 2))
    return pl.pallas_call(
        functools.partial(_moba_sample_attn_kernel, past=past),
        grid_spec=pltpu.PrefetchScalarGridSpec(
            num_scalar_prefetch=3,
            grid=(bs, H_B),
            in_specs=[row, row, row] + [page_spec(n) for n in range(N_SEL_PAGES)] * 2,
            out_specs=pl.BlockSpec((1, 1, 1, LANES), lambda bi, hi, pg, bl, sl: (bi, hi, 0, 0))),
        out_shape=jax.ShapeDtypeStruct((bs, H_B, 1, LANES), F32),
        compiler_params=_cparams(("parallel", "arbitrary")),
        name="moba_sample_attn",
    )(pages, blocks, slopes, q3, knew3, vnew3, *([cache_k2] * N_SEL_PAGES), *([cache_v2] * N_SEL_PAGES))


def _layer_norm(x, g, b):
    mu = jnp.mean(x, -1, keepdims=True)
    xc = x - mu
    var = jnp.mean(xc * xc, -1, keepdims=True)
    return xc * lax.rsqrt(var + LN_EPS) * g + b


def _mixer_kernel(x_ref, ya_ref, yb_ref, wm_ref, bm_ref, wa_ref, wb_ref, wo_ref, g_ref, b_ref, wr_ref, br_ref,
                  x1_ref, route_ref, *, alpha):
    x = x_ref[...]
    d = x.shape[-1]
    gates = _sigmoid(_dot(x.astype(BF16), wm_ref[...]) + bm_ref[...])
    m = (gates[:, :d] * _dot(ya_ref[...].astype(BF16), wa_ref[...])
         + gates[:, d:] * _dot(yb_ref[...].astype(BF16), wb_ref[...]))
    x1 = _layer_norm(alpha * x + _dot(m.astype(BF16), wo_ref[...]), g_ref[...], b_ref[...])
    x1_ref[...] = x1

    logits = _dot(x1, wr_ref[...], precision=HI) + br_ref[...]
    lane = lax.broadcasted_iota(I32, logits.shape, 1)
    lg = jnp.where(lane < N_GROUPS, logits, NEG_INF)
    mg = jnp.max(lg, -1, keepdims=True)
    p_grp = 1.0 / jnp.sum(jnp.exp(lg - mg), -1, keepdims=True)
    grp = jnp.min(jnp.where(lg == mg, lane, LANES), -1, keepdims=True)
    lo = N_GROUPS + grp * E_PER_GROUP
    le = jnp.where((lane >= lo) & (lane < lo + E_PER_GROUP), logits, NEG_INF)
    v1 = jnp.max(le, -1, keepdims=True)
    i1 = jnp.min(jnp.where(le == v1, lane, LANES), -1, keepdims=True)
    le2 = jnp.where(lane == i1, NEG_INF, le)
    v2 = jnp.max(le2, -1, keepdims=True)
    i2 = jnp.min(jnp.where(le2 == v2, lane, LANES), -1, keepdims=True)
    e = jnp.exp(v2 - v1)
    w0 = p_grp / (1.0 + e)
    w1 = p_grp * e / (1.0 + e)
    route = jnp.where(lane == 0, (i1 - N_GROUPS).astype(F32),
                      jnp.where(lane == 1, (i2 - N_GROUPS).astype(F32),
                                jnp.where(lane == 2, w0, jnp.where(lane == 3, w1, 0.0))))
    route_ref[...] = route


def _mixer(x2d, ya, yb, wts, tm, alpha):
    m, d = x2d.shape
    assert m % tm == 0
    row = lambda n: pl.BlockSpec((tm, n), lambda i: (i, 0))
    full = lambda a: pl.BlockSpec(a.shape, lambda i: (0, 0))
    return pl.pallas_call(
        functools.partial(_mixer_kernel, alpha=alpha),
        grid=(m // tm,),
        in_specs=[row(d), row(ya.shape[1]), row(yb.shape[1])] + [full(a) for a in wts],
        out_specs=[row(d), row(LANES)],
        out_shape=[jax.ShapeDtypeStruct((m, d), F32), jax.ShapeDtypeStruct((m, LANES), F32)],
        compiler_params=_cparams(("parallel",)),
        name="mixer",
    )(x2d, ya, yb, *wts)


def _rank_kernel(route_ref, carry_in_ref, rank_ref, counts_ref, carry_ref):
    i = pl.program_id(0)

    @pl.when(i == 0)
    def _():
        carry_ref[...] = carry_in_ref[...]

    tn = route_ref.shape[0]
    route = route_ref[...]
    lane = lax.broadcasted_iota(I32, (tn, LANES), 1)
    ri = lax.broadcasted_iota(I32, (tn, tn), 0)
    ci = lax.broadcasted_iota(I32, (tn, tn), 1)
    before = jnp.where(ri > ci, 1.0, 0.0).astype(BF16)
    carry = carry_ref[0:1, :]
    ranks = []
    for slot in range(TOPK_E):
        eid = jnp.sum(jnp.where(lane == slot, route, 0.0), -1, keepdims=True).astype(I32)
        hot = lane == eid
        hot_f = jnp.where(hot, 1.0, 0.0)
        earlier = _dot(before, hot_f.astype(BF16))
        ranks.append(jnp.sum(jnp.where(hot, earlier + carry, 0.0), -1, keepdims=True))
        carry = carry + jnp.sum(hot_f, 0, keepdims=True)
    carry_ref[...] = jnp.broadcast_to(carry, carry_ref.shape)
    rank_ref[...] = jnp.where(lane == 0, ranks[0], jnp.where(lane == 1, ranks[1], 0.0))
    counts_ref[...] = jnp.broadcast_to(carry, counts_ref.shape)


def _rank(route, carry_in, tn):
    m = route.shape[0]
    assert m % tn == 0
    row = pl.BlockSpec((tn, LANES), lambda i: (i, 0))
    one = pl.BlockSpec((SUBLANES, LANES), lambda i: (0, 0))
    return pl.pallas_call(
        _rank_kernel,
        grid=(m // tn,),
        in_specs=[row, one],
        out_specs=[row, one],
        out_shape=[jax.ShapeDtypeStruct((m, LANES), F32), jax.ShapeDtypeStruct((SUBLANES, LANES), F32)],
        scratch_shapes=[pltpu.VMEM((SUBLANES, LANES), F32)],
        compiler_params=_cparams(("arbitrary",)),
        name="moe_rank",
    )(route, carry_in)


def _dispatch_kernel(d0_ref, d1_ref, x_ref, xb_in_ref, xb_ref, sem):
    del xb_in_ref
    tm = x_ref.shape[0]

    def copy(t, dref):
        return pltpu.make_async_copy(x_ref.at[pl.ds(t, 1)], xb_ref.at[pl.ds(dref[t], 1)], sem)

    def issue(t, c):
        copy(t, d0_ref).start()
        copy(t, d1_ref).start()
        return c

    def drain(t, c):
        copy(t, d0_ref).wait()
        copy(t, d1_ref).wait()
        return c

    lax.fori_loop(0, tm, issue, 0)
    lax.fori_loop(0, tm, drain, 0)


def _dispatch(dest0, dest1, x1, xb, tm):
    m, d = x1.shape
    assert m % tm == 0
    idx = pl.BlockSpec((tm,), lambda i: (i,), memory_space=pltpu.SMEM)
    return pl.pallas_call(
        _dispatch_kernel,
        grid=(m // tm,),
        in_specs=[idx, idx, pl.BlockSpec((tm, d), lambda i: (i, 0)), pl.BlockSpec(memory_space=pl.ANY)],
        out_specs=pl.BlockSpec(memory_space=pl.ANY),
        out_shape=jax.ShapeDtypeStruct(xb.shape, xb.dtype),
        scratch_shapes=[pltpu.SemaphoreType.DMA(())],
        input_output_aliases={3: 0},
        compiler_params=_cparams(("arbitrary",)),
        name="moe_dispatch",
    )(dest0, dest1, x1, xb)


def _expert_kernel(be_ref, xb_ref, wg_ref, wu_ref, wd_ref, yb_ref):
    del be_ref
    x = xb_ref[...].astype(BF16)
    hmid = _silu(_dot(x, wg_ref[0])) * _dot(x, wu_ref[0])
    yb_ref[...] = _dot(hmid.astype(BF16), wd_ref[0])


def _experts(blk_e, xb, wg, wu, wd):
    p, d = xb.shape
    de = wg.shape[-1]
    nblk = p // MOE_ROWS
    row = pl.BlockSpec((MOE_ROWS, d), lambda i, be: (i, 0))
    return pl.pallas_call(
        _expert_kernel,
        grid_spec=pltpu.PrefetchScalarGridSpec(
            num_scalar_prefetch=1,
            grid=(nblk,),
            in_specs=[row,
                      pl.BlockSpec((1, d, de), lambda i, be: (be[i], 0, 0)),
                      pl.BlockSpec((1, d, de), lambda i, be: (be[i], 0, 0)),
                      pl.BlockSpec((1, de, d), lambda i, be: (be[i], 0, 0))],
            out_specs=row),
        out_shape=jax.ShapeDtypeStruct((p, d), F32),
        compiler_params=_cparams(("arbitrary",)),
        name="moe_experts",
    )(blk_e, xb, wg, wu, wd)


def _combine_kernel(d0_ref, d1_ref, h_ref, route_ref, g_ref, b_ref, yb_ref, o_ref, y0_ref, y1_ref, sem, *, alpha):
    tm = h_ref.shape[0]

    def copy(t, dref, buf):
        return pltpu.make_async_copy(yb_ref.at[pl.ds(dref[t], 1)], buf.at[pl.ds(t, 1)], sem)

    def issue(t, c):
        copy(t, d0_ref, y0_ref).start()
        copy(t, d1_ref, y1_ref).start()
        return c

    def drain(t, c):
        copy(t, d0_ref, y0_ref).wait()
        copy(t, d1_ref, y1_ref).wait()
        return c

    lax.fori_loop(0, tm, issue, 0)
    lax.fori_loop(0, tm, drain, 0)
    route = route_ref[...]
    w0 = route[:, 2:3]
    w1 = route[:, 3:4]
    moe = w0 * y0_ref[...] + w1 * y1_ref[...]
    o_ref[...] = _layer_norm(alpha * h_ref[...] + moe, g_ref[...], b_ref[...])


def _combine(dest0, dest1, h, route, ln_g, ln_b, yb, tm, alpha):
    m, d = h.shape
    assert m % tm == 0
    idx = pl.BlockSpec((tm,), lambda i: (i,), memory_space=pltpu.SMEM)
    vec = pl.BlockSpec((1, d), lambda i: (0, 0))
    return pl.pallas_call(
        functools.partial(_combine_kernel, alpha=alpha),
        grid=(m // tm,),
        in_specs=[idx, idx, pl.BlockSpec((tm, d), lambda i: (i, 0)), pl.BlockSpec((tm, LANES), lambda i: (i, 0)),
                  vec, vec, pl.BlockSpec(memory_space=pl.ANY)],
        out_specs=pl.BlockSpec((tm, d), lambda i: (i, 0)),
        out_shape=jax.ShapeDtypeStruct((m, d), F32),
        scratch_shapes=[pltpu.VMEM((tm, d), F32), pltpu.VMEM((tm, d), F32), pltpu.SemaphoreType.DMA(())],
        compiler_params=_cparams(("arbitrary",)),
        name="moe_combine",
    )(dest0, dest1, h, route, ln_g, ln_b, yb)


def _pad_lanes(v, offset=0):
    out = jnp.zeros((1, LANES), F32)
    return out.at[0, offset:offset + v.shape[0]].set(v.astype(F32))


def _moe_plan(route_p, route_s, tn_p):
    rank_p, counts_p = _rank(route_p, jnp.zeros((SUBLANES, LANES), F32), tn_p)
    rank_s, counts = _rank(route_s, counts_p, route_s.shape[0])
    counts = counts[0, :N_EXPERTS].astype(I32)
    pcounts = (counts + MOE_ROWS - 1) // MOE_ROWS * MOE_ROWS
    pend = jnp.cumsum(pcounts)
    pstart = pend - pcounts
    n_assign = (route_p.shape[0] + route_s.shape[0]) * TOPK_E
    nblk = n_assign // MOE_ROWS + N_EXPERTS
    blk_e = jnp.minimum(jnp.searchsorted(pend, jnp.arange(nblk, dtype=I32) * MOE_ROWS, side='right'),
                        N_EXPERTS - 1).astype(I32)

    def dests(route, rank):
        eid = route[:, :TOPK_E].astype(I32)
        d = pstart[eid] + rank[:, :TOPK_E].astype(I32)
        return d[:, 0], d[:, 1]

    return dests(route_p, rank_p), dests(route_s, rank_s), blk_e, nblk


def kernel(x_prompt, x_sample, cache_k, cache_v, page_table, state_conv, state_gdn, w_in, conv_w, a_log, dt_bias,
           gdn_norm, w_br_a, w_br_b, w_merge, b_merge, w_out, ln1_g, ln1_b, w_router_group, b_router_group,
           w_router_expert, b_router_expert, w_exp_gate, w_exp_up, w_exp_down, ln2_g, ln2_b):
    b, t, d = x_prompt.shape
    bs, ts, _ = x_sample.shape
    depth = w_in.shape[0]
    assert depth == 1 and ts == 1
    n_pages = page_table.shape[1]
    past = n_pages * PAGE_SIZE
    alpha = float((2 * depth) ** 0.25)
    slopes = jnp.exp2(-8.0 * jnp.arange(1, H_B + 1, dtype=F32) / H_B)
    l = 0

    wi = w_in[l]
    n_a = 4 * H_A * DK_A
    w_main = jnp.concatenate([wi[:, :n_a], wi[:, n_a + 2 * H_A:]], axis=1).astype(BF16)
    w_ab = jnp.pad(wi[:, n_a:n_a + 2 * H_A], ((0, 0), (0, LANES - 2 * H_A)))
    alog_row = _pad_lanes(a_log[l])
    dtb_row = _pad_lanes(dt_bias[l])
    gnorm_row = gdn_norm[l].reshape(1, DV_A)
    w_r = jnp.concatenate([w_router_group[l]] + [w_router_expert[l, g] for g in range(N_GROUPS)], axis=1)
    w_r = jnp.pad(w_r, ((0, 0), (0, LANES - w_r.shape[1])))
    b_r = _pad_lanes(jnp.concatenate([b_router_group[l], b_router_expert[l].reshape(-1)]))
    mixer_w = (w_merge[l].astype(BF16), b_merge[l].reshape(1, -1), w_br_a[l].astype(BF16), w_br_b[l].astype(BF16),
               w_out[l].astype(BF16), ln1_g[l].reshape(1, -1), ln1_b[l].reshape(1, -1), w_r, b_r)

    xp2 = x_prompt.reshape(b * t, d)
    qkv_p, z_p, ab_p, qb_p, kb_p, vb_p = _in_proj(xp2, w_main, w_ab, 512)
    cdim = qkv_p.shape[1]
    qkv_p3 = qkv_p.reshape(b, t, cdim)
    ya_p, gdn_p = _gdn_prompt(qkv_p3, z_p.reshape(b, t, -1), ab_p.reshape(b, t, LANES), conv_w[l],
                              alog_row, dtb_row, gnorm_row, 512)
    conv_p = qkv_p3[:, t - (CONV_W - 1):, :]
    yb_p = _moba_prompt(qb_p.reshape(b, t, -1), kb_p.reshape(b, t, -1), vb_p.reshape(b, t, -1), slopes)
    x1_p, route_p = _mixer(xp2, ya_p.reshape(b * t, -1), yb_p.reshape(b * t, -1), mixer_w, 512, alpha)

    xs2 = x_sample.reshape(bs, d)
    qkv_s, z_s, ab_s, qb_s, kb_s, vb_s = _in_proj(xs2, w_main, w_ab, bs)
    ya_s, gdn_s, conv_s = _gdn_sample(qkv_s, z_s, ab_s, state_conv[l], state_gdn[l], conv_w[l],
                                      alog_row, dtb_row, gnorm_row)
    width = H_B * HD_B
    cache_k2 = cache_k[l].reshape(-1, PAGE_SIZE, width)
    cache_v2 = cache_v[l].reshape(-1, PAGE_SIZE, width)
    q3 = qb_s.reshape(bs, 1, width)
    sel = _moba_sample_select(page_table, q3, cache_k2)[:, :MOBA_TOPK, :H_B]
    blocks = jnp.transpose(sel, (0, 2, 1))
    page_idx = blocks[..., None] * PAGES_PER_BLOCK + jnp.arange(PAGES_PER_BLOCK, dtype=I32)
    pages = jnp.take_along_axis(page_table, page_idx.reshape(bs, -1), axis=1)
    yb_s4 = _moba_sample_attn(pages.reshape(-1).astype(I32), blocks.reshape(-1).astype(I32), slopes, q3,
                              kb_s.reshape(bs, 1, width), vb_s.reshape(bs, 1, width), cache_k2, cache_v2, past)
    yb_s4 = yb_s4.reshape(bs, H_B // 2, 2, LANES)
    yb_s = (yb_s4[:, :, 0, :] + yb_s4[:, :, 1, :]).reshape(bs, width)
    x1_s, route_s = _mixer(xs2, ya_s.reshape(bs, -1), yb_s, mixer_w, bs, alpha)

    (d0_p, d1_p), (d0_s, d1_s), blk_e, nblk = _moe_plan(route_p, route_s, 512)
    xb = jnp.zeros((nblk * MOE_ROWS, d), F32)
    xb = _dispatch(d0_p, d1_p, x1_p, xb, 1024)
    xb = _dispatch(d0_s, d1_s, x1_s, xb, bs)
    yb = _experts(blk_e, xb, w_exp_gate[l].astype(BF16), w_exp_up[l].astype(BF16), w_exp_down[l].astype(BF16))
    g2, b2 = ln2_g[l].reshape(1, -1), ln2_b[l].reshape(1, -1)
    y_p = _combine(d0_p, d1_p, x1_p, route_p, g2, b2, yb, 1024, alpha)
    y_s = _combine(d0_s, d1_s, x1_s, route_s, g2, b2, yb, bs, alpha)

    hb = (H_B, HD_B)
    return (y_p.reshape(b, t, d), y_s.reshape(bs, ts, d),
            kb_p.reshape((1, b, t) + hb), vb_p.reshape((1, b, t) + hb),
            conv_p[None], gdn_p[None],
            kb_s.reshape((1, bs, ts) + hb), vb_s.reshape((1, bs, ts) + hb),
            conv_s[None], gdn_s[None])
```
